```python
import math
import jax, jax.numpy as jnp
from jax import lax
import numpy as np

D_MODEL = 1024
BATCH = 4
SEQ = 8192
DEPTH = 1
DEC_BATCH = 128
DEC_SEQ = 4
PAST_LEN = 16384
PAGE_SIZE = 128

D_MIX = D_MODEL
SSM_WIDTH = D_MIX // 2
SSM_HEAD_DIM = 64
SSM_HEADS = SSM_WIDTH // SSM_HEAD_DIM
SSM_GROUPS = 2
SSM_HPG = SSM_HEADS // SSM_GROUPS
D_STATE = 128
CONV_W = 4
CONV_DIM = SSM_WIDTH + 2 * SSM_GROUPS * D_STATE
SSD_CHUNK = 128
ATT_WIDTH = D_MIX - SSM_WIDTH
HEAD_DIM = 64
N_Q = ATT_WIDTH // HEAD_DIM
N_KV = 2
Q_PER_KV = N_Q // N_KV
WINDOW = 128
ROT_DIM = HEAD_DIM // 4
ROPE_THETA = 500000.0
D_FF = 4 * D_MODEL
LN_EPS = 1e-5
RMS_EPS = 1e-5
ALPHA = (2 * DEPTH) ** 0.25
BETA = (8 * DEPTH) ** -0.25

Z_END = SSM_WIDTH
XBC_END = Z_END + CONV_DIM
DT_END = XBC_END + SSM_HEADS
Q_END = DT_END + N_Q * HEAD_DIM
K_END = Q_END + N_KV * HEAD_DIM
D_IN_PROJ = K_END + N_KV * HEAD_DIM

kernel_name = 'hymba_ssd_swa_sink_deepnorm_adaln_step'


def window_rows():
    return min(WINDOW, PAST_LEN)


def layer_norm(x, g, b):
    xf = x.astype(jnp.float32)
    mu = jnp.mean(xf, -1, keepdims=True)
    var = jnp.mean(jnp.square(xf - mu), -1, keepdims=True)
    return ((xf - mu) * lax.rsqrt(var + LN_EPS) * g.astype(jnp.float32) + b.astype(jnp.float32)).astype(x.dtype)


def rotary(x, pos):
    half = ROT_DIM // 2
    inv = ROPE_THETA ** (-jnp.arange(half, dtype=jnp.float32) / half)
    ang = pos.astype(jnp.float32)[:, None] * inv[None, :]
    cos = jnp.cos(ang)[None, :, None, :]
    sin = jnp.sin(ang)[None, :, None, :]
    xr = x[..., :ROT_DIM].astype(jnp.float32)
    x1, x2 = xr[..., :half], xr[..., half:]
    rot = jnp.concatenate([x1 * cos - x2 * sin, x2 * cos + x1 * sin], -1)
    return jnp.concatenate([rot.astype(x.dtype), x[..., ROT_DIM:]], -1)


def causal_conv(u, buf, w, b):
    t = u.shape[1]
    up = jnp.concatenate([buf.astype(u.dtype), u], 1)
    y = up[:, 0:t] * w[0]
    for j in range(1, CONV_W):
        y = y + up[:, j:j + t] * w[j]
    return jax.nn.silu(y + b), up[:, t:]


def segsum(a):
    cs = jnp.cumsum(a, -1)
    n = a.shape[-1]
    mask = jnp.tril(jnp.ones((n, n), dtype=bool))
    return jnp.where(mask, cs[..., :, None] - cs[..., None, :], -jnp.inf)


def ssd_scan(x, dt, A, B, C, h0):
    b, t = x.shape[:2]
    l = math.gcd(t, SSD_CHUNK)
    c = t // l
    G, R, P, N = SSM_GROUPS, SSM_HPG, SSM_HEAD_DIM, D_STATE
    X = (x * dt[..., None]).reshape(b, c, l, G, R, P)
    a = (dt * A).reshape(b, c, l, G, R).transpose(0, 3, 4, 1, 2)
    Bc = B.reshape(b, c, l, G, N)
    Cc = C.reshape(b, c, l, G, N)
    a_cs = jnp.cumsum(a, -1)
    Lmat = jnp.exp(segsum(a))
    cb = jnp.einsum('bclgn,bcsgn->bgcls', Cc, Bc)
    y_diag = jnp.einsum('bgcls,bgrcls,bcsgrp->bclgrp', cb, Lmat, X)
    decay_states = jnp.exp(a_cs[..., -1:] - a_cs)
    states = jnp.einsum('bcsgn,bgrcs,bcsgrp->bcgrpn', Bc, decay_states, X)
    states = jnp.concatenate([h0.reshape(b, G, R, P, N)[:, None], states], 1)
    chunk_tot = jnp.pad(a_cs[..., -1], ((0, 0), (0, 0), (0, 0), (1, 0)))
    decay_chunk = jnp.exp(segsum(chunk_tot))
    states = jnp.einsum('bgrzc,bcgrpn->bzgrpn', decay_chunk, states)
    prev, final = states[:, :-1], states[:, -1]
    y_off = jnp.einsum('bclgn,bcgrpn,bgrcl->bclgrp', Cc, prev, jnp.exp(a_cs))
    y = (y_diag + y_off).reshape(b, t, SSM_HEADS, P)
    return y, final.reshape(b, SSM_HEADS, P, N)


def ssd_branch(z, xbc, dt_raw, conv_buf, h0, conv_w, conv_b, dt_bias, A_log, D_skip, norm_w):
    b, t = z.shape[:2]
    xbc, conv_new = causal_conv(xbc, conv_buf, conv_w, conv_b)
    xbc = xbc.astype(jnp.float32)
    xs = xbc[..., :SSM_WIDTH].reshape(b, t, SSM_HEADS, SSM_HEAD_DIM)
    Bm = xbc[..., SSM_WIDTH:SSM_WIDTH + SSM_GROUPS * D_STATE].reshape(b, t, SSM_GROUPS, D_STATE)
    Cm = xbc[..., SSM_WIDTH + SSM_GROUPS * D_STATE:].reshape(b, t, SSM_GROUPS, D_STATE)
    dt = jax.nn.softplus(dt_raw.astype(jnp.float32) + dt_bias.astype(jnp.float32))
    A = -jnp.exp(A_log.astype(jnp.float32))
    y, h_new = ssd_scan(xs, dt, A, Bm, Cm, h0.astype(jnp.float32))
    y = y + xs * D_skip.astype(jnp.float32)[:, None]
    y = y.reshape(b, t, SSM_WIDTH) * jax.nn.silu(z.astype(jnp.float32))
    yg = y.reshape(b, t, SSM_GROUPS, SSM_WIDTH // SSM_GROUPS)
    yg = yg * lax.rsqrt(jnp.mean(jnp.square(yg), -1, keepdims=True) + RMS_EPS)
    y = yg.reshape(b, t, SSM_WIDTH) * norm_w.astype(jnp.float32)
    return y.astype(z.dtype), conv_new, h_new


def band_mask(qpos, kpos):
    return (kpos <= qpos) & (kpos > qpos - WINDOW) & (kpos >= 0)


def sink_softmax(s, sink):
    m = jnp.maximum(jnp.max(s, -1, keepdims=True), sink)
    p = jnp.exp(s - m)
    return p / (jnp.sum(p, -1, keepdims=True) + jnp.exp(sink - m))


def swa_prompt(q, k, v, sinks):
    b, t = q.shape[:2]
    nb = t // WINDOW
    qb = q.astype(jnp.float32).reshape(b, nb, WINDOW, N_KV, Q_PER_KV, HEAD_DIM)
    kb = k.astype(jnp.float32).reshape(b, nb, WINDOW, N_KV, HEAD_DIM)
    vb = v.astype(jnp.float32).reshape(b, nb, WINDOW, N_KV, HEAD_DIM)
    pad = ((0, 0), (1, 0), (0, 0), (0, 0), (0, 0))
    kk = jnp.concatenate([jnp.pad(kb, pad)[:, :-1], kb], 2)
    vv = jnp.concatenate([jnp.pad(vb, pad)[:, :-1], vb], 2)
    qpos = jnp.arange(t).reshape(nb, WINDOW)
    kpos = qpos[:, :1] - WINDOW + jnp.arange(2 * WINDOW)[None, :]
    mask = band_mask(qpos[:, :, None], kpos[:, None, :])
    s = jnp.einsum('bnqgrd,bnkgd->bngrqk', qb, kk) * (HEAD_DIM ** -0.5)
    s = jnp.where(mask[None, :, None, None], s, -jnp.inf)
    p = sink_softmax(s, sinks.astype(jnp.float32).reshape(N_KV, Q_PER_KV)[:, :, None, None])
    o = jnp.einsum('bngrqk,bnkgd->bnqgrd', p, vv)
    return o.reshape(b, t, ATT_WIDTH).astype(q.dtype)


def swa_decode(q, k, v, k_buf, v_buf, pos, sinks):
    b, t = q.shape[:2]
    wb = k_buf.shape[1]
    kk = jnp.concatenate([k_buf.astype(k.dtype), k], 1)
    vv = jnp.concatenate([v_buf.astype(v.dtype), v], 1)
    kpos = pos[0] - wb + jnp.arange(wb + t)
    mask = band_mask(pos[:, None], kpos[None, :])
    qg = q.astype(jnp.float32).reshape(b, t, N_KV, Q_PER_KV, HEAD_DIM)
    s = jnp.einsum('bqgrd,bkgd->bgrqk', qg, kk.astype(jnp.float32)) * (HEAD_DIM ** -0.5)
    s = jnp.where(mask, s, -jnp.inf)
    p = sink_softmax(s, sinks.astype(jnp.float32).reshape(N_KV, Q_PER_KV)[:, :, None, None])
    o = jnp.einsum('bgrqk,bkgd->bqgrd', p, vv.astype(jnp.float32))
    return o.reshape(b, t, ATT_WIDTH).astype(q.dtype), kk[:, t:], vv[:, t:]


def decoder_layer(x, c, pos, conv_buf, h0, k_buf, v_buf, w_ada, b_ada, w_in, conv_w, conv_b,
                  dt_bias, A_log, D_skip, ssm_norm_w, sinks, w_out, ln1_g, ln1_b,
                  w_up, w_down, ln2_g, ln2_b):
    b, t, _ = x.shape
    ada = jnp.einsum('bd,de->be', jax.nn.silu(c), w_ada) + b_ada
    sh_m, sc_m, g_m, sh_f, sc_f, g_f = jnp.split(ada[:, None, :], 6, axis=-1)
    h = x * (1.0 + sc_m) + sh_m
    p = jnp.einsum('btd,de->bte', h, w_in)
    z = p[..., :Z_END]
    xbc = p[..., Z_END:XBC_END]
    dt_raw = p[..., XBC_END:DT_END]
    q = rotary(p[..., DT_END:Q_END].reshape(b, t, N_Q, HEAD_DIM), pos)
    k = rotary(p[..., Q_END:K_END].reshape(b, t, N_KV, HEAD_DIM), pos)
    v = p[..., K_END:].reshape(b, t, N_KV, HEAD_DIM)
    if conv_buf is None:
        conv_buf = jnp.zeros((b, CONV_W - 1, CONV_DIM), x.dtype)
        h0 = jnp.zeros((b, SSM_HEADS, SSM_HEAD_DIM, D_STATE), jnp.float32)
    y_ssm, conv_new, h_new = ssd_branch(z, xbc, dt_raw, conv_buf, h0, conv_w, conv_b,
                                        dt_bias, A_log, D_skip, ssm_norm_w)
    if k_buf is None:
        wb = window_rows()
        y_att = swa_prompt(q, k, v, sinks)
        k_new, v_new = k[:, t - wb:], v[:, t - wb:]
    else:
        y_att, k_new, v_new = swa_decode(q, k, v, k_buf, v_buf, pos, sinks)
    mixed = jnp.einsum('bte,ed->btd', jnp.concatenate([y_ssm, y_att], -1), w_out)
    x = layer_norm(ALPHA * x + g_m * mixed, ln1_g, ln1_b)
    h = x * (1.0 + sc_f) + sh_f
    u = jnp.square(jax.nn.relu(jnp.einsum('btd,df->btf', h, w_up)))
    f = jnp.einsum('btf,fd->btd', u, w_down)
    x = layer_norm(ALPHA * x + g_f * f, ln2_g, ln2_b)
    return x, conv_new, h_new, k_new, v_new


def setup_inputs(seed: int = 0) -> dict:
    key = jax.random.key(seed)
    ks = jax.random.split(key, 32)
    f32 = jnp.float32

    def nrm(k, shape, s=1.0):
        return jax.random.normal(k, shape, f32) * s

    wb = window_rows()
    dt0 = jnp.exp(jax.random.uniform(ks[20], (DEPTH, SSM_HEADS), f32, math.log(1e-3), math.log(1e-1)))
    return {
        'x_prompt': nrm(ks[0], (BATCH, SEQ, D_MODEL)),
        'x_sample': nrm(ks[1], (DEC_BATCH, DEC_SEQ, D_MODEL)),
        'state_conv': nrm(ks[2], (DEPTH, DEC_BATCH, CONV_W - 1, CONV_DIM)),
        'state_ssm': nrm(ks[3], (DEPTH, DEC_BATCH, SSM_HEADS, SSM_HEAD_DIM, D_STATE), 0.3),
        'cache_k': nrm(ks[4], (DEPTH, DEC_BATCH, wb, N_KV, HEAD_DIM)),
        'cache_v': nrm(ks[5], (DEPTH, DEC_BATCH, wb, N_KV, HEAD_DIM)),
        'c_prompt': nrm(ks[6], (BATCH, D_MODEL)),
        'c_sample': nrm(ks[7], (DEC_BATCH, D_MODEL)),
        'ln_in_g': 1.0 + nrm(ks[8], (D_MODEL,), 0.02),
        'ln_in_b': nrm(ks[9], (D_MODEL,), 0.02),
        'w_ada': nrm(ks[10], (DEPTH, D_MODEL, 6 * D_MODEL), 0.5 * D_MODEL ** -0.5),
        'b_ada': nrm(ks[11], (DEPTH, 6 * D_MODEL), 0.02),
        'w_in': nrm(ks[12], (DEPTH, D_MODEL, D_IN_PROJ), D_MODEL ** -0.5),
        'conv_w': nrm(ks[13], (DEPTH, CONV_W, CONV_DIM), CONV_W ** -0.5),
        'conv_b': nrm(ks[14], (DEPTH, CONV_DIM), 0.02),
        'dt_bias': dt0 + jnp.log(-jnp.expm1(-dt0)),
        'A_log': jnp.log(jax.random.uniform(ks[15], (DEPTH, SSM_HEADS), f32, 1.0, 16.0)),
        'D_skip': 1.0 + nrm(ks[16], (DEPTH, SSM_HEADS), 0.02),
        'ssm_norm_w': 1.0 + nrm(ks[17], (DEPTH, SSM_WIDTH), 0.02),
        'sinks': nrm(ks[18], (DEPTH, N_Q)),
        'w_out': nrm(ks[19], (DEPTH, D_MIX, D_MODEL), BETA * D_MIX ** -0.5),
        'ln1_g': 1.0 + nrm(ks[21], (DEPTH, D_MODEL), 0.02),
        'ln1_b': nrm(ks[22], (DEPTH, D_MODEL), 0.02),
        'w_up': nrm(ks[23], (DEPTH, D_MODEL, D_FF), D_MODEL ** -0.5),
        'w_down': nrm(ks[24], (DEPTH, D_FF, D_MODEL), BETA * D_FF ** -0.5),
        'ln2_g': 1.0 + nrm(ks[25], (DEPTH, D_MODEL), 0.02),
        'ln2_b': nrm(ks[26], (DEPTH, D_MODEL), 0.02),
    }


def reference(x_prompt, x_sample, state_conv, state_ssm, cache_k, cache_v, c_prompt, c_sample,
              ln_in_g, ln_in_b, w_ada, b_ada, w_in, conv_w, conv_b, dt_bias, A_log, D_skip,
              ssm_norm_w, sinks, w_out, ln1_g, ln1_b, w_up, w_down, ln2_g, ln2_b):
    pos_p = jnp.arange(x_prompt.shape[1], dtype=jnp.int32)
    pos_s = PAST_LEN + jnp.arange(x_sample.shape[1], dtype=jnp.int32)
    xp = layer_norm(x_prompt, ln_in_g, ln_in_b)
    xs = layer_norm(x_sample, ln_in_g, ln_in_b)
    conv_p, ssm_p, k_p, v_p = [], [], [], []
    conv_s, ssm_s, k_s, v_s = [], [], [], []
    for l in range(DEPTH):
        lw = (w_ada[l], b_ada[l], w_in[l], conv_w[l], conv_b[l], dt_bias[l], A_log[l], D_skip[l],
              ssm_norm_w[l], sinks[l], w_out[l], ln1_g[l], ln1_b[l], w_up[l], w_down[l], ln2_g[l], ln2_b[l])
        xp, cp, hp, kp, vp = decoder_layer(xp, c_prompt, pos_p, None, None, None, None, *lw)
        xs, cs, hs, kq, vq = decoder_layer(xs, c_sample, pos_s, state_conv[l], state_ssm[l],
                                           cache_k[l], cache_v[l], *lw)
        conv_p.append(cp); ssm_p.append(hp); k_p.append(kp); v_p.append(vp)
        conv_s.append(cs); ssm_s.append(hs); k_s.append(kq); v_s.append(vq)
    return (xp, xs, jnp.stack(conv_p), jnp.stack(ssm_p), jnp.stack(k_p), jnp.stack(v_p),
            jnp.stack(conv_s), jnp.stack(ssm_s), jnp.stack(k_s), jnp.stack(v_s))
```

```python
import functools

import numpy as np
import jax
import jax.numpy as jnp
from jax import lax
from jax.experimental import pallas as pl
from jax.experimental.pallas import tpu as pltpu

F32 = jnp.float32
BF16 = jnp.bfloat16

D_MODEL = 1024
SSM_WIDTH = 512
SSM_HEAD_DIM = 64
SSM_HEADS = 8
SSM_GROUPS = 2
D_STATE = 128
CONV_W = 4
CONV_DIM = SSM_WIDTH + 2 * SSM_GROUPS * D_STATE
ATT_WIDTH = 512
HEAD_DIM = 64
N_Q = 8
N_KV = 2
WINDOW = 128
ROT_DIM = 16
ROPE_THETA = 500000.0
D_FF = 4096
LN_EPS = 1e-5
RMS_EPS = 1e-5
PAST_LEN = 16384
ALPHA = 2.0 ** 0.25

Z_END = SSM_WIDTH
XBC_END = Z_END + CONV_DIM
DT_END = XBC_END + SSM_HEADS
QKV_W = (N_Q + 2 * N_KV) * HEAD_DIM

LANE = 128
SUBLANE = 8
CHUNK = 128
TB = 512
FF_CHUNK = 1024
TILE_T = SUBLANE
TOK0 = 3
SB = 16
DENSE_ROWS = 512
VMEM_LIMIT = 60 * 1024 * 1024

NEG_INF = float("-inf")

V_LN_IN_G, V_LN_IN_B, V_LN1_G, V_LN1_B, V_LN2_G, V_LN2_B, V_CONV_B = range(7)
A_SH_M, A_SC_M, A_G_M, A_SH_F, A_SC_F, A_G_F = range(6)


def _dot(a, b):
    return jnp.dot(a, b, preferred_element_type=F32)


def _dot_nt(a, b):
    return lax.dot_general(a, b, (((1,), (1,)), ((), ())), preferred_element_type=F32)


def _split_parts(a, passes):
    parts = []
    r = a
    for i in range(passes):
        p = r.astype(BF16)
        parts.append(p)
        if i + 1 < passes:
            r = r - p.astype(F32)
    return parts


def _sel_dot_r(a, m, passes):
    out = None
    for p in _split_parts(a, passes):
        t = _dot(p, m)
        out = t if out is None else out + t
    return out


def _sel_dot_l(m, a, passes):
    out = None
    for p in _split_parts(a, passes):
        t = _dot(m, p)
        out = t if out is None else out + t
    return out


def _layer_norm(x, g, b):
    mu = jnp.mean(x, -1, keepdims=True)
    xc = x - mu
    var = jnp.mean(xc * xc, -1, keepdims=True)
    return xc * lax.rsqrt(var + LN_EPS) * g + b


def _sigmoid(x):
    return 1.0 / (1.0 + jnp.exp(-x))


def _silu(x):
    return x * _sigmoid(x)


def _softplus(x):
    return jnp.maximum(x, 0.0) + jnp.log1p(jnp.exp(-jnp.abs(x)))


def _rotary(x, cos, sin_a, sin_b):
    half = ROT_DIM // 2
    tiles = []
    for j in range(x.shape[1] // LANE):
        t = x[:, j * LANE:(j + 1) * LANE]
        tiles.append(t * cos + pltpu.roll(t, half, 1) * sin_a + pltpu.roll(t, LANE - half, 1) * sin_b)
    return tiles[0] if len(tiles) == 1 else jnp.concatenate(tiles, axis=1)


def _split_heads(q):
    lo = (lax.broadcasted_iota(jnp.int32, q.shape, 1) & (LANE - 1)) < HEAD_DIM
    return jnp.concatenate([jnp.where(lo, q, 0.0), jnp.where(lo, 0.0, q)], axis=1)


def _dup_halves(x):
    lane = lax.broadcasted_iota(jnp.int32, x.shape, 1)
    sw = pltpu.roll(x, HEAD_DIM, 1)
    lo = lane < HEAD_DIM
    return jnp.where(lo, x, sw), jnp.where(lo, sw, x)


def _conv_silu(ext_rows, convw, conv_b, n_rows, first):
    acc = ext_rows[first:first + n_rows] * convw[CONV_W - 1:CONV_W]
    for k in range(1, CONV_W):
        sh = pltpu.roll(ext_rows, k, 0)[first:first + n_rows]
        acc = acc + sh * convw[CONV_W - 1 - k:CONV_W - k]
    return _silu(acc + conv_b)


def _ssd_diag(c_bf, b_bf, x_bf, cs, cs_t, mask):
    n = cs.shape[0]
    lane = lax.broadcasted_iota(jnp.int32, (n, LANE), 1)
    outs = []
    for g in range(SSM_GROUPS):
        cb = _dot_nt(c_bf[:, g * D_STATE:(g + 1) * D_STATE], b_bf[:, g * D_STATE:(g + 1) * D_STATE])
        for pr in range(2):
            ms = []
            for h in (4 * g + 2 * pr, 4 * g + 2 * pr + 1):
                diff = cs[:, h:h + 1] - cs_t[h:h + 1, :]
                lm = jnp.exp(jnp.where(mask, diff, NEG_INF))
                ms.append((cb * lm).astype(BF16))
            col = (2 * g + pr) * LANE
            y2 = _dot(jnp.concatenate(ms, axis=0), x_bf[:, col:col + LANE])
            outs.append(jnp.where(lane < SSM_HEAD_DIM, y2[:n], y2[n:]))
    return jnp.concatenate(outs, axis=1)


def _ssm_gate_norm(y, xs, z, d_exp, norm_w):
    y = (y + xs * d_exp) * _silu(z)
    half = SSM_WIDTH // SSM_GROUPS
    parts = []
    for g in range(SSM_GROUPS):
        yg = y[:, g * half:(g + 1) * half]
        parts.append(yg * lax.rsqrt(jnp.mean(yg * yg, -1, keepdims=True) + RMS_EPS))
    return jnp.concatenate(parts, axis=1) * norm_w


def _dense_out(xn, ymix_bf, g_m, sh_f, sc_f, g_f, vec, wout_ref, wup_ref, wdn_ref):
    mixed = _dot(ymix_bf, wout_ref[...])
    x1 = _layer_norm(ALPHA * xn + g_m * mixed, vec[V_LN1_G:V_LN1_G + 1], vec[V_LN1_B:V_LN1_B + 1])
    h2 = (x1 * (1.0 + sc_f) + sh_f).astype(BF16)
    f = None
    for j in range(D_FF // FF_CHUNK):
        u = _dot(h2, wup_ref[:, j * FF_CHUNK:(j + 1) * FF_CHUNK])
        u = jnp.maximum(u, 0.0)
        t = _dot((u * u).astype(BF16), wdn_ref[j * FF_CHUNK:(j + 1) * FF_CHUNK, :])
        f = t if f is None else f + t
    return _layer_norm(ALPHA * x1 + g_f * f, vec[V_LN2_G:V_LN2_G + 1], vec[V_LN2_B:V_LN2_B + 1])


def _ada_kernel(c_ref, w_ref, b_ref, o_ref):
    h = _silu(c_ref[...]).astype(BF16)
    o_ref[0] = _dot(h, w_ref[...]) + b_ref[...]


def _ada_call(c_all, w_ada_bf, b_ada):
    rows = c_all.shape[0]
    return pl.pallas_call(
        _ada_kernel,
        grid=(6,),
        in_specs=[
            pl.BlockSpec((rows, D_MODEL), lambda j: (0, 0)),
            pl.BlockSpec((D_MODEL, D_MODEL), lambda j: (0, j)),
            pl.BlockSpec((1, D_MODEL), lambda j: (0, j)),
        ],
        out_specs=pl.BlockSpec((1, rows, D_MODEL), lambda j: (j, 0, 0)),
        out_shape=jax.ShapeDtypeStruct((6, rows, D_MODEL), F32),
        compiler_params=pltpu.CompilerParams(dimension_semantics=("arbitrary",)),
    )(c_all, w_ada_bf, b_ada)


def _prompt_kernel(sinks_ref, x_ref, ada_ref, cos_ref, sa_ref, sb_ref, vec_ref, convw_ref, v512_ref,
                   v128_ref, wzx_ref, wqkv_ref, wdt_ref, wout_ref, wup_ref, wdn_ref, tri_ref, e_ref,
                   et_ref,
                   y_ref, conv_ref, ssm_ref, kp_ref, vp_ref,
                   carry_ref, state_ref, kd_ref, vd_ref, q_ref, c_ref, b_ref, xb_ref, xd_ref, cs_ref,
                   ecs_ref, yssd_ref, yatt_ref):
    b = pl.program_id(0)
    t = pl.program_id(1)
    nt = pl.num_programs(1)

    @pl.when(t == 0)
    def _():
        carry_ref[...] = jnp.zeros_like(carry_ref)
        state_ref[...] = jnp.zeros_like(state_ref)
        kd_ref[0:CHUNK, :] = jnp.zeros((CHUNK, 2 * LANE), BF16)
        vd_ref[0:CHUNK, :] = jnp.zeros((CHUNK, 2 * LANE), BF16)

    vec = vec_ref[...]

    def ada(i):
        return ada_ref[i, pl.ds(b, 1), :]

    xn = _layer_norm(x_ref[0], vec[V_LN_IN_G:V_LN_IN_G + 1], vec[V_LN_IN_B:V_LN_IN_B + 1])
    h = (xn * (1.0 + ada(A_SC_M)) + ada(A_SH_M)).astype(BF16)
    zx = _dot(h, wzx_ref[...])
    qkv = _dot(h, wqkv_ref[...])
    dtr = _dot(h, wdt_ref[...])
    z = zx[:, :Z_END]
    xbc = zx[:, Z_END:]

    ext = jnp.concatenate([carry_ref[...], xbc], axis=0)
    carry_ref[...] = xbc[TB - SUBLANE:, :]
    xbc_c = _conv_silu(ext, convw_ref[...], vec[V_CONV_B:V_CONV_B + 1], TB, SUBLANE)
    xs = xbc_c[:, :SSM_WIDTH]
    b_ref[...] = xbc_c[:, SSM_WIDTH:SSM_WIDTH + 2 * D_STATE].astype(BF16)
    c_ref[...] = xbc_c[:, SSM_WIDTH + 2 * D_STATE:].astype(BF16)

    v128 = v128_ref[...]
    dt = _softplus(dtr + v128[0:1])
    a = dt * (-jnp.exp(v128[1:2]))
    cs = _sel_dot_l(tri_ref[...], a, 3)
    cs3 = cs.reshape(TB // CHUNK, CHUNK, LANE)
    cs_last = jnp.broadcast_to(cs3[:, CHUNK - 1:CHUNK, :], cs3.shape).reshape(TB, LANE)
    e = e_ref[...]
    dt_exp = _sel_dot_r(dt, e, 2)
    xdt = xs * dt_exp
    cs_ref[...] = cs
    ecs_ref[...] = _sel_dot_r(jnp.exp(cs), e, 2)
    xb_ref[...] = xdt.astype(BF16)
    xd_ref[...] = xdt * _sel_dot_r(jnp.exp(cs_last - cs), e, 2)

    cos, sa, sb = cos_ref[...], sa_ref[...], sb_ref[...]
    q = _rotary(qkv[:, :ATT_WIDTH], cos, sa, sb) * (HEAD_DIM ** -0.5)
    k = _rotary(qkv[:, ATT_WIDTH:ATT_WIDTH + LANE], cos, sa, sb)
    v = qkv[:, ATT_WIDTH + LANE:]
    q_ref[...] = _split_heads(q).astype(BF16)
    k0, k1 = _dup_halves(k)
    v0, v1 = _dup_halves(v)
    kd_ref[CHUNK:, :] = jnp.concatenate([k0, k1], axis=1).astype(BF16)
    vd_ref[CHUNK:, :] = jnp.concatenate([v0, v1], axis=1).astype(BF16)

    ii = lax.broadcasted_iota(jnp.int32, (CHUNK, CHUNK), 0)
    jj = lax.broadcasted_iota(jnp.int32, (CHUNK, CHUNK), 1)
    tril = jj <= ii
    lane = lax.broadcasted_iota(jnp.int32, (CHUNK, LANE), 1)
    et = et_ref[...]

    def chunk_body(c, carry):
        r0 = pl.multiple_of(c * CHUNK, CHUNK)
        rows = pl.ds(r0, CHUNK)
        cs_c = cs_ref[rows, :]
        cs_t = cs_c.T
        c_bf = c_ref[rows, :]
        b_bf = b_ref[rows, :]
        ydiag = _ssd_diag(c_bf, b_bf, xb_ref[rows, :], cs_c, cs_t, tril)
        st = state_ref[...]
        st_bf = st.astype(BF16)
        half = SSM_WIDTH // SSM_GROUPS
        yoff = jnp.concatenate(
            [_dot_nt(c_bf[:, g * D_STATE:(g + 1) * D_STATE], st_bf[g * half:(g + 1) * half, :])
             for g in range(SSM_GROUPS)], axis=1)
        yssd_ref[rows, :] = ydiag + yoff * ecs_ref[rows, :]
        dec_last = jnp.broadcast_to(jnp.exp(cs_t[:, CHUNK - 1:CHUNK]), (LANE, D_STATE))
        row_dec = _sel_dot_l(et, dec_last, 2)
        xd = xd_ref[rows, :]
        ds = jnp.concatenate(
            [_dot(xd[:, g * half:(g + 1) * half].T.astype(BF16), b_bf[:, g * D_STATE:(g + 1) * D_STATE])
             for g in range(SSM_GROUPS)], axis=0)
        state_ref[...] = row_dec * st + ds

        gblk = t * (TB // CHUNK) + c
        thr = jnp.where(gblk > 0, 0, 4 * CHUNK)
        bias = jnp.concatenate([jnp.where((jj - ii) > thr, 0.0, NEG_INF),
                                jnp.where(tril, 0.0, NEG_INF)], axis=1)
        krows = pl.ds(r0, 2 * CHUNK)
        outs = []
        for g in range(N_KV):
            lhs = []
            for pr in range(2):
                col = (2 * g + pr) * LANE
                lhs.append(q_ref[rows, col:col + LANE])
                lhs.append(q_ref[rows, ATT_WIDTH + col:ATT_WIDTH + col + LANE])
            s = _dot_nt(jnp.concatenate(lhs, axis=0), kd_ref[krows, g * LANE:(g + 1) * LANE])
            ps, inv = [], []
            for hh in range(4):
                sink = sinks_ref[4 * g + hh]
                sh = s[hh * CHUNK:(hh + 1) * CHUNK] + bias
                m = jnp.maximum(jnp.max(sh, -1, keepdims=True), sink)
                p = jnp.exp(sh - m)
                inv.append(1.0 / (jnp.sum(p, -1, keepdims=True) + jnp.exp(sink - m)))
                ps.append(p.astype(BF16))
            o = _dot(jnp.concatenate(ps, axis=0), vd_ref[krows, g * LANE:(g + 1) * LANE])
            for pr in range(2):
                oa = o[(2 * pr) * CHUNK:(2 * pr + 1) * CHUNK] * inv[2 * pr]
                ob = o[(2 * pr + 1) * CHUNK:(2 * pr + 2) * CHUNK] * inv[2 * pr + 1]
                outs.append(jnp.where(lane < HEAD_DIM, oa, ob))
        yatt_ref[rows, :] = jnp.concatenate(outs, axis=1).astype(BF16)
        return carry

    lax.fori_loop(0, TB // CHUNK, chunk_body, 0)

    kd_ref[0:CHUNK, :] = kd_ref[TB:TB + CHUNK, :]
    vd_ref[0:CHUNK, :] = vd_ref[TB:TB + CHUNK, :]

    v512 = v512_ref[...]
    y_ssm = _ssm_gate_norm(yssd_ref[...], xs, z, v512[0:1], v512[1:2])
    ymix = jnp.concatenate([y_ssm.astype(BF16), yatt_ref[...]], axis=1)
    y_ref[0] = _dense_out(xn, ymix, ada(A_G_M), ada(A_SH_F), ada(A_SC_F), ada(A_G_F), vec,
                          wout_ref, wup_ref, wdn_ref)

    @pl.when(t == nt - 1)
    def _():
        conv_ref[0] = xbc[TB - SUBLANE:, :]
        ssm_ref[0] = state_ref[...]
        kp_ref[0] = k[TB - WINDOW:, :]
        vp_ref[0] = v[TB - WINDOW:, :]


def _const_spec(shape):
    nd = len(shape)
    return pl.BlockSpec(shape, lambda *_: (0,) * nd, pipeline_mode=pl.Buffered(1))


def _prompt_call(x, ada_all, ada_blk, sinks, tabs, vec, convw, v512, v128, w, consts):
    bsz, seq, _ = x.shape
    nt = seq // TB
    cos, sa, sb = tabs
    wzx, wqkv, wdt, wout, wup, wdn = w
    tri, e, et = consts
    in_specs = [
        pl.BlockSpec(memory_space=pltpu.SMEM),
        pl.BlockSpec((1, TB, D_MODEL), lambda b, t: (b, t, 0)),
        pl.BlockSpec((6, SUBLANE, D_MODEL), lambda b, t: (0, ada_blk, 0), pipeline_mode=pl.Buffered(1)),
        pl.BlockSpec((TB, LANE), lambda b, t: (t, 0)),
        pl.BlockSpec((TB, LANE), lambda b, t: (t, 0)),
        pl.BlockSpec((TB, LANE), lambda b, t: (t, 0)),
    ] + [_const_spec(a.shape) for a in (vec, convw, v512, v128, wzx, wqkv, wdt, wout, wup, wdn, tri, e, et)]
    out_shape = (
        jax.ShapeDtypeStruct((bsz, seq, D_MODEL), F32),
        jax.ShapeDtypeStruct((bsz, SUBLANE, CONV_DIM), F32),
        jax.ShapeDtypeStruct((bsz, SSM_WIDTH, D_STATE), F32),
        jax.ShapeDtypeStruct((bsz, WINDOW, LANE), F32),
        jax.ShapeDtypeStruct((bsz, WINDOW, LANE), F32),
    )
    out_specs = (
        pl.BlockSpec((1, TB, D_MODEL), lambda b, t: (b, t, 0)),
        pl.BlockSpec((1, SUBLANE, CONV_DIM), lambda b, t: (b, 0, 0)),
        pl.BlockSpec((1, SSM_WIDTH, D_STATE), lambda b, t: (b, 0, 0)),
        pl.BlockSpec((1, WINDOW, LANE), lambda b, t: (b, 0, 0)),
        pl.BlockSpec((1, WINDOW, LANE), lambda b, t: (b, 0, 0)),
    )
    scratch = [
        pltpu.VMEM((SUBLANE, CONV_DIM), F32),
        pltpu.VMEM((SSM_WIDTH, D_STATE), F32),
        pltpu.VMEM((TB + CHUNK, 2 * LANE), BF16),
        pltpu.VMEM((TB + CHUNK, 2 * LANE), BF16),
        pltpu.VMEM((TB, 2 * ATT_WIDTH), BF16),
        pltpu.VMEM((TB, 2 * D_STATE), BF16),
        pltpu.VMEM((TB, 2 * D_STATE), BF16),
        pltpu.VMEM((TB, SSM_WIDTH), BF16),
        pltpu.VMEM((TB, SSM_WIDTH), F32),
        pltpu.VMEM((TB, LANE), F32),
        pltpu.VMEM((TB, SSM_WIDTH), F32),
        pltpu.VMEM((TB, SSM_WIDTH), F32),
        pltpu.VMEM((TB, ATT_WIDTH), BF16),
    ]
    return pl.pallas_call(
        _prompt_kernel,
        grid=(bsz, nt),
        in_specs=in_specs,
        out_specs=out_specs,
        out_shape=out_shape,
        scratch_shapes=scratch,
        compiler_params=pltpu.CompilerParams(
            dimension_semantics=("arbitrary", "arbitrary"), vmem_limit_bytes=VMEM_LIMIT),
    )(sinks, x, ada_all, cos, sa, sb, vec, convw, v512, v128, wzx, wqkv, wdt, wout, wup, wdn, tri, e, et)


def _row_ada(ada_ref, rsel, i):
    return _sel_dot_r_left(rsel, ada_ref[i])


def _sel_dot_r_left(rsel, a):
    return _sel_dot_l(rsel, a, 3)


def _sample_in_kernel(x_ref, ada_ref, rsel_ref, vec_ref, wzx_ref, wqkv_ref, wdt_ref,
                      xn_ref, zx_ref, qkv_ref, dtr_ref):
    vec = vec_ref[...]
    rsel = rsel_ref[...]
    xn = _layer_norm(x_ref[...], vec[V_LN_IN_G:V_LN_IN_G + 1], vec[V_LN_IN_B:V_LN_IN_B + 1])
    h = (xn * (1.0 + _row_ada(ada_ref, rsel, A_SC_M)) + _row_ada(ada_ref, rsel, A_SH_M)).astype(BF16)
    xn_ref[...] = xn
    zx_ref[...] = _dot(h, wzx_ref[...])
    qkv_ref[...] = _dot(h, wqkv_ref[...])
    dtr_ref[...] = _dot(h, wdt_ref[...])


def _sample_in_call(x2, ada_all, rsel, vec, wzx, wqkv, wdt):
    rows = x2.shape[0]
    nb = DENSE_ROWS // TILE_T
    row_spec = lambda w: pl.BlockSpec((DENSE_ROWS, w), lambda i: (i, 0))
    return pl.pallas_call(
        _sample_in_kernel,
        grid=(rows // DENSE_ROWS,),
        in_specs=[row_spec(D_MODEL),
                  pl.BlockSpec((6, nb, D_MODEL), lambda i: (0, i, 0))]
                 + [_const_spec(a.shape) for a in (rsel, vec, wzx, wqkv, wdt)],
        out_specs=(row_spec(D_MODEL), row_spec(XBC_END), row_spec(QKV_W), row_spec(LANE)),
        out_shape=(jax.ShapeDtypeStruct((rows, D_MODEL), F32),
                   jax.ShapeDtypeStruct((rows, XBC_END), F32),
                   jax.ShapeDtypeStruct((rows, QKV_W), F32),
                   jax.ShapeDtypeStruct((rows, LANE), F32)),
        compiler_params=pltpu.CompilerParams(
            dimension_semantics=("arbitrary",), vmem_limit_bytes=VMEM_LIMIT),
    )(x2, ada_all, rsel, vec, wzx, wqkv, wdt)


def _sample_out_kernel(xn_ref, ymix_ref, ada_ref, rsel_ref, vec_ref, wout_ref, wup_ref, wdn_ref, y_ref):
    rsel = rsel_ref[...]
    y_ref[...] = _dense_out(
        xn_ref[...], ymix_ref[...],
        _row_ada(ada_ref, rsel, A_G_M), _row_ada(ada_ref, rsel, A_SH_F),
        _row_ada(ada_ref, rsel, A_SC_F), _row_ada(ada_ref, rsel, A_G_F),
        vec_ref[...], wout_ref, wup_ref, wdn_ref)


def _sample_out_call(xn, ymix, ada_all, rsel, vec, wout, wup, wdn):
    rows = xn.shape[0]
    nb = DENSE_ROWS // TILE_T
    row_spec = lambda w: pl.BlockSpec((DENSE_ROWS, w), lambda i: (i, 0))
    return pl.pallas_call(
        _sample_out_kernel,
        grid=(rows // DENSE_ROWS,),
        in_specs=[row_spec(D_MODEL), row_spec(D_MODEL),
                  pl.BlockSpec((6, nb, D_MODEL), lambda i: (0, i, 0))]
                 + [_const_spec(a.shape) for a in (rsel, vec, wout, wup, wdn)],
        out_specs=row_spec(D_MODEL),
        out_shape=jax.ShapeDtypeStruct((rows, D_MODEL), F32),
        compiler_params=pltpu.CompilerParams(
            dimension_semantics=("arbitrary",), vmem_limit_bytes=VMEM_LIMIT),
    )(xn, ymix, ada_all, rsel, vec, wout, wup, wdn)


def _sample_mix_kernel(sinks_ref, zx_ref, qkv_ref, dtr_ref, cst_ref, h0_ref, ck_ref, cv_ref,
                       cos_ref, sa_ref, sb_ref, vec_ref, convw_ref, v512_ref, v128_ref,
                       tri_ref, e_ref, et_ref,
                       ymix_ref, convn_ref, hn_ref, ko_ref, vo_ref):
    rows = SB * TILE_T
    vec = vec_ref[...]
    zx = zx_ref[...]
    z = zx[:, :Z_END]
    xbc = zx[:, Z_END:]
    trow = lax.broadcasted_iota(jnp.int32, (rows, 1), 0) & (TILE_T - 1)
    is_tok = (trow >= TOK0) & (trow < TOK0 + 4)

    ext = jnp.where(is_tok, xbc, cst_ref[...])
    convn_ref[...] = pltpu.roll(ext, rows - (TOK0 + 1), 0)
    xbc_c = _conv_silu(ext, convw_ref[...], vec[V_CONV_B:V_CONV_B + 1], rows, 0)
    xs = xbc_c[:, :SSM_WIDTH]
    b_f = xbc_c[:, SSM_WIDTH:SSM_WIDTH + 2 * D_STATE]
    c_f = xbc_c[:, SSM_WIDTH + 2 * D_STATE:]
    b_bf = b_f.astype(BF16)
    c_bf = c_f.astype(BF16)

    v128 = v128_ref[...]
    dt = jnp.where(is_tok, _softplus(dtr_ref[...] + v128[0:1]), 0.0)
    a = dt * (-jnp.exp(v128[1:2]))
    cs = _sel_dot_l(tri_ref[...], a, 3)
    cs3 = cs.reshape(SB, TILE_T, LANE)
    cs_end = jnp.broadcast_to(cs3[:, TOK0 + 3:TOK0 + 4, :], cs3.shape).reshape(rows, LANE)
    e = e_ref[...]
    xdt = xs * _sel_dot_r(dt, e, 2)
    ecs_exp = _sel_dot_r(jnp.exp(cs), e, 2)
    xd = xdt * _sel_dot_r(jnp.exp(cs_end - cs), e, 2)
    cs_t = cs.T

    ii = lax.broadcasted_iota(jnp.int32, (rows, rows), 0)
    jj = lax.broadcasted_iota(jnp.int32, (rows, rows), 1)
    same_causal = ((ii >> 3) == (jj >> 3)) & (jj <= ii)
    ydiag = _ssd_diag(c_bf, b_bf, xdt.astype(BF16), cs, cs_t, same_causal)

    half = SSM_WIDTH // SSM_GROUPS
    c3 = jnp.concatenate([c_f[:, g * D_STATE:(g + 1) * D_STATE].reshape(SB, TILE_T, D_STATE)
                          for g in range(SSM_GROUPS)], axis=1).astype(BF16)
    yy = jnp.einsum("bqd,bkd->bqk", c3, h0_ref[...].astype(BF16), preferred_element_type=F32)
    yoff = jnp.concatenate(
        [yy[:, g * TILE_T:(g + 1) * TILE_T, g * half:(g + 1) * half].reshape(rows, half)
         for g in range(SSM_GROUPS)], axis=1)
    y = ydiag + yoff * ecs_exp
    v512 = v512_ref[...]
    y_ssm = _ssm_gate_norm(y, xs, z, v512[0:1], v512[1:2])

    ecs_t = jnp.exp(cs_t)
    row_dec = _sel_dot_l(et_ref[...], ecs_t, 2)
    brow = lax.broadcasted_iota(jnp.int32, (rows, 1), 0) >> 3
    xd_t = [xd[:, g * half:(g + 1) * half].T.astype(BF16) for g in range(SSM_GROUPS)]
    for bb in range(SB):
        col = bb * TILE_T + TOK0 + 3
        dec = jnp.broadcast_to(row_dec[:, col:col + 1], (SSM_WIDTH, D_STATE))
        ds = jnp.concatenate(
            [_dot(xd_t[g], jnp.where(brow == bb, b_f[:, g * D_STATE:(g + 1) * D_STATE], 0.0).astype(BF16))
             for g in range(SSM_GROUPS)], axis=0)
        hn_ref[bb] = dec * h0_ref[bb] + ds

    qkv = qkv_ref[...]
    cos, sa, sb = cos_ref[...], sa_ref[...], sb_ref[...]
    q = _rotary(qkv[:, :ATT_WIDTH], cos, sa, sb) * (HEAD_DIM ** -0.5)
    k = _rotary(qkv[:, ATT_WIDTH:ATT_WIDTH + LANE], cos, sa, sb)
    v = qkv[:, ATT_WIDTH + LANE:]
    kn = [x.astype(BF16) for x in _dup_halves(k)]
    vn = [x.astype(BF16) for x in _dup_halves(v)]
    ck = ck_ref[...]
    cv = cv_ref[...]
    ck2 = ck.reshape(SB * WINDOW, LANE)
    cv2 = cv.reshape(SB * WINDOW, LANE)
    kc = [x.reshape(SB, WINDOW, LANE).astype(BF16) for x in _dup_halves(ck2)]
    vc = [x.reshape(SB, WINDOW, LANE).astype(BF16) for x in _dup_halves(cv2)]

    srow = 4 * rows
    r_i = lax.broadcasted_iota(jnp.int32, (srow, 1), 0)
    r_t = r_i & (TILE_T - 1)
    r_h = (r_i >> 3) & 3
    r_b = r_i >> 5
    j_c = lax.broadcasted_iota(jnp.int32, (srow, WINDOW), 1)
    cache_ok = j_c > (r_t - TOK0)
    j_n = lax.broadcasted_iota(jnp.int32, (srow, rows), 1)
    jn_t = j_n & (TILE_T - 1)
    new_ok = ((j_n >> 3) == r_b) & (jn_t <= r_t) & (jn_t >= TOK0) & (jn_t < TOK0 + 4)
    lane_q = lax.broadcasted_iota(jnp.int32, (rows, LANE), 1)
    lo = lane_q < HEAD_DIM
    q_ab = _split_heads(q)
    outs = []
    for g in range(N_KV):
        parts = []
        for pr in range(2):
            col = (2 * g + pr) * LANE
            parts.append(q_ab[:, col:col + LANE].reshape(SB, TILE_T, LANE))
            parts.append(q_ab[:, ATT_WIDTH + col:ATT_WIDTH + col + LANE].reshape(SB, TILE_T, LANE))
        q3 = jnp.concatenate(parts, axis=1)
        q2 = q3.reshape(srow, LANE).astype(BF16)
        s_c = jnp.einsum("bqd,bkd->bqk", q3.astype(BF16), kc[g],
                         preferred_element_type=F32).reshape(srow, WINDOW)
        s_n = _dot_nt(q2, kn[g])
        s_c = jnp.where(cache_ok, s_c, NEG_INF)
        s_n = jnp.where(new_ok, s_n, NEG_INF)
        sink = jnp.where(r_h == 0, sinks_ref[4 * g],
                         jnp.where(r_h == 1, sinks_ref[4 * g + 1],
                                   jnp.where(r_h == 2, sinks_ref[4 * g + 2], sinks_ref[4 * g + 3])))
        m = jnp.maximum(jnp.maximum(jnp.max(s_c, -1, keepdims=True), jnp.max(s_n, -1, keepdims=True)), sink)
        p_c = jnp.exp(s_c - m)
        p_n = jnp.exp(s_n - m)
        den = jnp.sum(p_c, -1, keepdims=True) + jnp.sum(p_n, -1, keepdims=True) + jnp.exp(sink - m)
        o = jnp.einsum("bqk,bkd->bqd", p_c.reshape(SB, 4 * TILE_T, WINDOW).astype(BF16), vc[g],
                       preferred_element_type=F32).reshape(srow, LANE)
        o = (o + _dot(p_n.astype(BF16), vn[g])) * (1.0 / den)
        o3 = o.reshape(SB, 4 * TILE_T, LANE)
        for pr in range(2):
            oa = o3[:, (2 * pr) * TILE_T:(2 * pr + 1) * TILE_T, :].reshape(rows, LANE)
            ob = o3[:, (2 * pr + 1) * TILE_T:(2 * pr + 2) * TILE_T, :].reshape(rows, LANE)
            outs.append(jnp.where(lo, oa, ob))
    y_att = jnp.concatenate(outs, axis=1)
    ymix_ref[...] = jnp.concatenate([y_ssm, y_att], axis=1).astype(BF16)

    tail = lax.broadcasted_iota(jnp.int32, (SB, TILE_T, LANE), 1) >= TILE_T - 4
    for src2, new2, dst in ((ck2, k, ko_ref), (cv2, v, vo_ref)):
        sh = pltpu.roll(src2, SB * WINDOW - 4, 0).reshape(SB, WINDOW, LANE)
        new3 = pltpu.roll(new2, 1, 0).reshape(SB, TILE_T, LANE)
        last = jnp.where(tail, new3, sh[:, WINDOW - TILE_T:, :])
        dst[...] = jnp.concatenate([sh[:, :WINDOW - TILE_T, :], last], axis=1)


def _sample_mix_call(sinks, zx, qkv, dtr, cst, h0, ck, cv, tabs, vec, convw, v512, v128, consts):
    nb = h0.shape[0]
    rows = SB * TILE_T
    cos, sa, sb = tabs
    tri, e, et = consts
    row_spec = lambda w: pl.BlockSpec((rows, w), lambda i: (i, 0))
    in_specs = [
        pl.BlockSpec(memory_space=pltpu.SMEM),
        row_spec(XBC_END), row_spec(QKV_W), row_spec(LANE), row_spec(CONV_DIM),
        pl.BlockSpec((SB, SSM_WIDTH, D_STATE), lambda i: (i, 0, 0)),
        pl.BlockSpec((SB, WINDOW, LANE), lambda i: (i, 0, 0)),
        pl.BlockSpec((SB, WINDOW, LANE), lambda i: (i, 0, 0)),
    ] + [_const_spec(a.shape) for a in (cos, sa, sb, vec, convw, v512, v128, tri, e, et)]
    out_specs = (
        row_spec(D_MODEL), row_spec(CONV_DIM),
        pl.BlockSpec((SB, SSM_WIDTH, D_STATE), lambda i: (i, 0, 0)),
        pl.BlockSpec((SB, WINDOW, LANE), lambda i: (i, 0, 0)),
        pl.BlockSpec((SB, WINDOW, LANE), lambda i: (i, 0, 0)),
    )
    out_shape = (
        jax.ShapeDtypeStruct((nb * TILE_T, D_MODEL), BF16),
        jax.ShapeDtypeStruct((nb * TILE_T, CONV_DIM), F32),
        jax.ShapeDtypeStruct((nb, SSM_WIDTH, D_STATE), F32),
        jax.ShapeDtypeStruct((nb, WINDOW, LANE), F32),
        jax.ShapeDtypeStruct((nb, WINDOW, LANE), F32),
    )
    return pl.pallas_call(
        _sample_mix_kernel,
        grid=(nb // SB,),
        in_specs=in_specs,
        out_specs=out_specs,
        out_shape=out_shape,
        compiler_params=pltpu.CompilerParams(
            dimension_semantics=("arbitrary",), vmem_limit_bytes=VMEM_LIMIT),
    )(sinks, zx, qkv, dtr, cst, h0, ck, cv, cos, sa, sb, vec, convw, v512, v128, tri, e, et)


def _rot_tables(pos):
    half = ROT_DIM // 2
    inv = ROPE_THETA ** (-jnp.arange(half, dtype=F32) / half)
    ang = pos.astype(F32)[:, None] * inv[None, :]
    cos, sin = jnp.cos(ang), jnp.sin(ang)
    n = pos.shape[0]
    one = jnp.ones((n, HEAD_DIM - ROT_DIM), F32)
    zero = jnp.zeros((n, HEAD_DIM - ROT_DIM), F32)
    zh = jnp.zeros((n, half), F32)
    cos_h = jnp.concatenate([cos, cos, one], axis=1)
    sa_h = jnp.concatenate([zh, sin, zero], axis=1)
    sb_h = jnp.concatenate([-sin, zh, zero], axis=1)
    return tuple(jnp.concatenate([t, t], axis=1) for t in (cos_h, sa_h, sb_h))


def _selectors(chunk_rows, seg):
    r = np.arange(chunk_rows)
    tri = ((r[:, None] // seg) == (r[None, :] // seg)) & (r[None, :] <= r[:, None])
    e = np.zeros((LANE, SSM_WIDTH), np.float32)
    ch = np.arange(SSM_WIDTH)
    e[ch // SSM_HEAD_DIM, ch] = 1.0
    return (jnp.asarray(tri.astype(np.float32), BF16), jnp.asarray(e, BF16), jnp.asarray(e.T.copy(), BF16))


def kernel(x_prompt, x_sample, state_conv, state_ssm, cache_k, cache_v, c_prompt, c_sample, ln_in_g, ln_in_b, w_ada, b_ada, w_in, conv_w, conv_b, dt_bias, A_log, D_skip, ssm_norm_w, sinks, w_out, ln1_g, ln1_b, w_up, w_down, ln2_g, ln2_b):
    bsz, seq, _ = x_prompt.shape
    nb, dec_t, _ = x_sample.shape
    assert seq % TB == 0 and nb % SB == 0 and (nb * TILE_T) % DENSE_ROWS == 0
    assert dec_t == 4 and bsz <= SUBLANE and nb % SUBLANE == 0 and cache_k.shape[2] == WINDOW

    w_in0 = w_in[0]
    wzx = w_in0[:, :XBC_END].astype(BF16)
    wdt = jnp.pad(w_in0[:, XBC_END:DT_END], ((0, 0), (0, LANE - SSM_HEADS))).astype(BF16)
    wqkv = w_in0[:, DT_END:].astype(BF16)
    wout = w_out[0].astype(BF16)
    wup = w_up[0].astype(BF16)
    wdn = w_down[0].astype(BF16)
    vec = jnp.stack([ln_in_g, ln_in_b, ln1_g[0], ln1_b[0], ln2_g[0], ln2_b[0], conv_b[0],
                     jnp.zeros((D_MODEL,), F32)])
    convw = jnp.pad(conv_w[0], ((0, SUBLANE - CONV_W), (0, 0)))
    v512 = jnp.pad(jnp.stack([jnp.repeat(D_skip[0], SSM_HEAD_DIM), ssm_norm_w[0]]),
                   ((0, SUBLANE - 2), (0, 0)))
    v128 = jnp.pad(jnp.stack([dt_bias[0], A_log[0]]), ((0, SUBLANE - 2), (0, LANE - SSM_HEADS)))
    sinks0 = sinks[0]

    c_all = jnp.concatenate([c_sample, c_prompt, jnp.zeros((SUBLANE - bsz, D_MODEL), F32)], axis=0)
    ada_all = _ada_call(c_all, w_ada[0].astype(BF16), b_ada)

    tabs_p = _rot_tables(jnp.arange(seq, dtype=jnp.int32))
    consts_p = _selectors(TB, CHUNK)
    y_p, conv_p8, ssm_p, k_p, v_p = _prompt_call(
        x_prompt, ada_all, nb // SUBLANE, sinks0, tabs_p, vec, convw, v512, v128,
        (wzx, wqkv, wdt, wout, wup, wdn), consts_p)

    rows_s = nb * TILE_T
    x_s = jnp.pad(x_sample, ((0, 0), (TOK0, TILE_T - TOK0 - dec_t), (0, 0))).reshape(rows_s, D_MODEL)
    cst = jnp.pad(state_conv[0], ((0, 0), (0, TILE_T - (CONV_W - 1)), (0, 0))).reshape(rows_s, CONV_DIM)
    tile_pos = jnp.clip(jnp.arange(TILE_T, dtype=jnp.int32) - TOK0, 0, dec_t - 1) + PAST_LEN
    tabs_s = tuple(jnp.tile(t, (SB, 1)) for t in _rot_tables(tile_pos))
    consts_s = _selectors(SB * TILE_T, TILE_T)
    rr = np.arange(DENSE_ROWS)
    rsel = jnp.asarray((rr[:, None] // TILE_T == np.arange(DENSE_ROWS // TILE_T)[None, :])
                       .astype(np.float32), BF16)
    xn_s, zx_s, qkv_s, dtr_s = _sample_in_call(x_s, ada_all, rsel, vec, wzx, wqkv, wdt)
    ymix_s, convn_s, ssm_s, k_s, v_s = _sample_mix_call(
        sinks0, zx_s, qkv_s, dtr_s, cst,
        state_ssm[0].reshape(nb, SSM_WIDTH, D_STATE),
        cache_k[0].reshape(nb, WINDOW, LANE), cache_v[0].reshape(nb, WINDOW, LANE),
        tabs_s, vec, convw, v512, v128, consts_s)
    y_s = _sample_out_call(xn_s, ymix_s, ada_all, rsel, vec, wout, wup, wdn)

    wb = WINDOW
    return (
        y_p,
        y_s.reshape(nb, TILE_T, D_MODEL)[:, TOK0:TOK0 + dec_t],
        conv_p8[:, SUBLANE - (CONV_W - 1):][None],
        ssm_p.reshape(1, bsz, SSM_HEADS, SSM_HEAD_DIM, D_STATE),
        k_p.reshape(1, bsz, wb, N_KV, HEAD_DIM),
        v_p.reshape(1, bsz, wb, N_KV, HEAD_DIM),
        convn_s.reshape(nb, TILE_T, CONV_DIM)[:, :CONV_W - 1][None],
        ssm_s.reshape(1, nb, SSM_HEADS, SSM_HEAD_DIM, D_STATE),
        k_s.reshape(1, nb, wb, N_KV, HEAD_DIM),
        v_s.reshape(1, nb, wb, N_KV, HEAD_DIM),
    )
```

```python
import functools

import numpy as np
import jax
import jax.numpy as jnp
from jax import lax
from jax.experimental import pallas as pl
from jax.experimental.pallas import tpu as pltpu

F32 = jnp.float32
BF16 = jnp.bfloat16

D_MODEL = 1024
SSM_WIDTH = 512
SSM_HEAD_DIM = 64
SSM_HEADS = 8
SSM_GROUPS = 2
D_STATE = 128
CONV_W = 4
CONV_DIM = SSM_WIDTH + 2 * SSM_GROUPS * D_STATE
ATT_WIDTH = 512
HEAD_DIM = 64
N_Q = 8
N_KV = 2
WINDOW = 128
ROT_DIM = 16
ROPE_THETA = 500000.0
D_FF = 4096
LN_EPS = 1e-5
RMS_EPS = 1e-5
PAST_LEN = 16384
ALPHA = 2.0 ** 0.25

Z_END = SSM_WIDTH
XBC_END = Z_END + CONV_DIM
DT_END = XBC_END + SSM_HEADS
QKV_W = (N_Q + 2 * N_KV) * HEAD_DIM

LANE = 128
SUBLANE = 8
CHUNK = 128
TB = 512
FF_PIECE = 512
PIECE_ORDER = "oiioxii" + 4 * "oioioioi" + "io"
TILE_T = SUBLANE
TOK0 = 3
DEC_T = 4
SB = 16
VMEM_LIMIT = 60 * 1024 * 1024

NEG_INF = float("-inf")

V_LN_IN_G, V_LN_IN_B, V_LN1_G, V_LN1_B, V_LN2_G, V_LN2_B, V_CONV_B = range(7)
A_SH_M, A_SC_M, A_G_M, A_SH_F, A_SC_F, A_G_F = range(6)


def _dot(a, b):
    return jnp.dot(a, b, preferred_element_type=F32)


def _dot_nt(a, b):
    return lax.dot_general(a, b, (((1,), (1,)), ((), ())), preferred_element_type=F32)


def _sel_dot_r(a, m, passes):
    out = None
    r = a
    for i in range(passes):
        p = r.astype(BF16)
        t = _dot(p, m)
        out = t if out is None else out + t
        if i + 1 < passes:
            r = r - p.astype(F32)
    return out


def _layer_norm(x, g, b):
    mu = jnp.mean(x, -1, keepdims=True)
    xc = x - mu
    var = jnp.mean(xc * xc, -1, keepdims=True)
    return xc * lax.rsqrt(var + LN_EPS) * g + b


def _silu(x):
    return x * (1.0 / (1.0 + jnp.exp(-x)))


def _softplus(x):
    return jnp.maximum(x, 0.0) + jnp.log1p(jnp.exp(-jnp.abs(x)))


def _rotary(x, cos, sin_a, sin_b):
    half = ROT_DIM // 2
    tiles = []
    for j in range(x.shape[1] // LANE):
        t = x[:, j * LANE:(j + 1) * LANE]
        tiles.append(t * cos + pltpu.roll(t, half, 1) * sin_a + pltpu.roll(t, LANE - half, 1) * sin_b)
    return tiles[0] if len(tiles) == 1 else jnp.concatenate(tiles, axis=1)


def _split_heads(q):
    lo = (lax.broadcasted_iota(jnp.int32, q.shape, 1) & (LANE - 1)) < HEAD_DIM
    return jnp.concatenate([jnp.where(lo, q, 0.0), jnp.where(lo, 0.0, q)], axis=1)


def _dup_halves(x):
    lane = lax.broadcasted_iota(jnp.int32, x.shape, 1)
    sw = pltpu.roll(x, HEAD_DIM, 1)
    lo = lane < HEAD_DIM
    return jnp.where(lo, x, sw), jnp.where(lo, sw, x)


def _conv_silu(ext_rows, convw, conv_b, n_rows, first):
    acc = ext_rows[first:first + n_rows] * convw[CONV_W - 1:CONV_W]
    for k in range(1, CONV_W):
        sh = pltpu.roll(ext_rows, k, 0)[first:first + n_rows]
        acc = acc + sh * convw[CONV_W - 1 - k:CONV_W - k]
    return _silu(acc + conv_b)


def _seg_cumsum(a, seg):
    row = lax.broadcasted_iota(jnp.int32, (a.shape[0], 1), 0) & (seg - 1)
    s = 1
    while s < seg:
        a = a + jnp.where(row >= s, pltpu.roll(a, s, 0), 0.0)
        s *= 2
    return a


def _ssd_diag(c_bf, b_bf, x_bf, cs, cs_t, mask):
    n = cs.shape[0]
    lane = lax.broadcasted_iota(jnp.int32, (n, LANE), 1)
    outs = []
    for g in range(SSM_GROUPS):
        cb = _dot_nt(c_bf[:, g * D_STATE:(g + 1) * D_STATE], b_bf[:, g * D_STATE:(g + 1) * D_STATE])
        for pr in range(2):
            ms = []
            for h in (4 * g + 2 * pr, 4 * g + 2 * pr + 1):
                diff = cs[:, h:h + 1] - cs_t[h:h + 1, :]
                lm = jnp.exp(jnp.where(mask, diff, NEG_INF))
                ms.append((cb * lm).astype(BF16))
            col = (2 * g + pr) * LANE
            y2 = _dot(jnp.concatenate(ms, axis=0), x_bf[:, col:col + LANE])
            outs.append(jnp.where(lane < SSM_HEAD_DIM, y2[:n], y2[n:]))
    return jnp.concatenate(outs, axis=1)


def _ssm_gate_norm(y, xs, z, d_exp, norm_w):
    y = (y + xs * d_exp) * _silu(z)
    half = SSM_WIDTH // SSM_GROUPS
    parts = []
    for g in range(SSM_GROUPS):
        yg = y[:, g * half:(g + 1) * half]
        parts.append(yg * lax.rsqrt(jnp.mean(yg * yg, -1, keepdims=True) + RMS_EPS))
    return jnp.concatenate(parts, axis=1) * norm_w


def _stage_out(xn, ymix_bf, ada, vec, wout_ref, wup_ref, wdn_ref, y_ref):
    mixed = _dot(ymix_bf, wout_ref[...])
    yield
    x1 = _layer_norm(ALPHA * xn + ada(A_G_M) * mixed, vec[V_LN1_G:V_LN1_G + 1], vec[V_LN1_B:V_LN1_B + 1])
    h2 = (x1 * (1.0 + ada(A_SC_F)) + ada(A_SH_F)).astype(BF16)
    yield
    f = None
    for j in range(D_FF // FF_PIECE):
        u = jnp.maximum(_dot(h2, wup_ref[:, j * FF_PIECE:(j + 1) * FF_PIECE]), 0.0)
        u = (u * u).astype(BF16)
        yield
        t = _dot(u, wdn_ref[j * FF_PIECE:(j + 1) * FF_PIECE, :])
        f = t if f is None else f + t
        yield
    y_ref[...] = _layer_norm(ALPHA * x1 + ada(A_G_F) * f, vec[V_LN2_G:V_LN2_G + 1],
                             vec[V_LN2_B:V_LN2_B + 1]).reshape(y_ref.shape)
    yield


def _const_spec(shape):
    nd = len(shape)
    return pl.BlockSpec(shape, lambda *_: (0,) * nd, pipeline_mode=pl.Buffered(1))


def _ada_kernel(c_ref, w_ref, b_ref, o_ref):
    h = _silu(c_ref[...]).astype(BF16)
    o_ref[0] = _dot(h, w_ref[...].astype(BF16)) + b_ref[...]


def _ada_call(c_all, w_ada, b_ada):
    rows = c_all.shape[0]
    return pl.pallas_call(
        _ada_kernel,
        grid=(6,),
        in_specs=[
            pl.BlockSpec((rows, D_MODEL), lambda j: (0, 0)),
            pl.BlockSpec((D_MODEL, D_MODEL), lambda j: (0, j)),
            pl.BlockSpec((1, D_MODEL), lambda j: (0, j)),
        ],
        out_specs=pl.BlockSpec((1, rows, D_MODEL), lambda j: (j, 0, 0)),
        out_shape=jax.ShapeDtypeStruct((6, rows, D_MODEL), F32),
        compiler_params=pltpu.CompilerParams(dimension_semantics=("arbitrary",)),
    )(c_all, w_ada, b_ada)


_STAGE_BUFS = (
    ("xn", D_MODEL, F32), ("z", SSM_WIDTH, F32), ("xs", SSM_WIDTH, F32),
    ("c", 2 * D_STATE, F32), ("b", 2 * D_STATE, F32), ("dt", LANE, F32), ("cs", LANE, F32),
    ("q", 2 * ATT_WIDTH, BF16), ("kd", 2 * LANE, BF16), ("vd", 2 * LANE, BF16),
)


def _stage_in(x, ada, tabs, vec, convw_ref, v128_ref, w_refs, carry_ref, buf, res):
    wzx_ref, wqkv_ref, wdt_ref = w_refs
    xn = _layer_norm(x, vec[V_LN_IN_G:V_LN_IN_G + 1], vec[V_LN_IN_B:V_LN_IN_B + 1])
    buf["xn"][...] = xn
    h = (xn * (1.0 + ada(A_SC_M)) + ada(A_SH_M)).astype(BF16)
    yield
    zx = _dot(h, wzx_ref[...])
    qkv = _dot(h, wqkv_ref[...])
    dtr = _dot(h, wdt_ref[...])
    yield
    buf["z"][...] = zx[:, :Z_END]
    xbc = zx[:, Z_END:]
    ext = jnp.concatenate([carry_ref[...], xbc], axis=0)
    tail = xbc[TB - SUBLANE:, :]
    carry_ref[...] = tail
    xbc_c = _conv_silu(ext, convw_ref[...], vec[V_CONV_B:V_CONV_B + 1], TB, SUBLANE)
    buf["xs"][...] = xbc_c[:, :SSM_WIDTH]
    buf["b"][...] = xbc_c[:, SSM_WIDTH:SSM_WIDTH + 2 * D_STATE]
    buf["c"][...] = xbc_c[:, SSM_WIDTH + 2 * D_STATE:]

    v128 = v128_ref[...]
    dt = _softplus(dtr + v128[0:1])
    buf["dt"][...] = dt
    buf["cs"][...] = _seg_cumsum(dt * (-jnp.exp(v128[1:2])), CHUNK)
    yield

    cos, sa, sb = tabs()
    q = _rotary(qkv[:, :ATT_WIDTH], cos, sa, sb) * (HEAD_DIM ** -0.5)
    k = _rotary(qkv[:, ATT_WIDTH:ATT_WIDTH + LANE], cos, sa, sb)
    v = qkv[:, ATT_WIDTH + LANE:]
    buf["q"][...] = _split_heads(q).astype(BF16)
    buf["kd"][...] = jnp.concatenate(_dup_halves(k), axis=1).astype(BF16)
    buf["vd"][...] = jnp.concatenate(_dup_halves(v), axis=1).astype(BF16)
    res.update(tail=tail, k=k, v=v)
    yield


def _mix_chunk(c, buf, state_ref, kprev_ref, vprev_ref, sinks_ref, first_thr, v512, ymix_ref):
    rows = slice(c * CHUNK, (c + 1) * CHUNK)
    ii = lax.broadcasted_iota(jnp.int32, (CHUNK, CHUNK), 0)
    jj = lax.broadcasted_iota(jnp.int32, (CHUNK, CHUNK), 1)
    tril = jj <= ii
    lane = lax.broadcasted_iota(jnp.int32, (CHUNK, LANE), 1)
    sub = lax.broadcasted_iota(jnp.int32, (CHUNK, LANE), 0)

    cs = buf["cs"][rows, :]
    dt = buf["dt"][rows, :]
    cs_t = cs.T
    dt_t = dt.T
    ecs = jnp.exp(cs)
    w_end = dt * jnp.exp(cs[CHUNK - 1:CHUNK, :] - cs)
    xs = buf["xs"][rows, :]
    x_t = xs.T.astype(BF16)
    c_f = buf["c"][rows, :]
    b_f = buf["b"][rows, :]
    c_bf = c_f.astype(BF16)
    b_bf = b_f.astype(BF16)
    st = state_ref[...]
    st_bf = st.astype(BF16)
    ys, new_state = [], []
    cbs = [_dot_nt(c_bf[:, g * D_STATE:(g + 1) * D_STATE], b_bf[:, g * D_STATE:(g + 1) * D_STATE])
           for g in range(SSM_GROUPS)]
    yield
    for g in range(SSM_GROUPS):
        gs = slice(g * D_STATE, (g + 1) * D_STATE)
        cb = cbs[g]
        for pr in range(2):
            pair = 2 * g + pr
            prow = slice(pair * LANE, (pair + 1) * LANE)
            lhs, bw = [], []
            for h in (2 * pair, 2 * pair + 1):
                diff = cs[:, h:h + 1] - cs_t[h:h + 1, :]
                m1 = cb * jnp.exp(jnp.where(tril, diff, NEG_INF)) * dt_t[h:h + 1, :]
                m2 = c_f[:, gs] * ecs[:, h:h + 1]
                lhs.append(jnp.concatenate([m1.astype(BF16), m2.astype(BF16)], axis=1))
                bw.append((b_f[:, gs] * w_end[:, h:h + 1]).astype(BF16))
            rhs_t = jnp.concatenate([x_t[prow, :], st_bf[prow, :]], axis=1)
            y2 = _dot_nt(jnp.concatenate(lhs, axis=0), rhs_t)
            ys.append(jnp.where(lane < SSM_HEAD_DIM, y2[:CHUNK], y2[CHUNK:]))
            ds2 = _dot(x_t[prow, :], jnp.concatenate(bw, axis=1))
            ds = jnp.where(sub < SSM_HEAD_DIM, ds2[:, :D_STATE], ds2[:, D_STATE:])
            dec = jnp.where(sub < SSM_HEAD_DIM, ecs[CHUNK - 1:CHUNK, 2 * pair:2 * pair + 1],
                            ecs[CHUNK - 1:CHUNK, 2 * pair + 1:2 * pair + 2])
            new_state.append(dec * st[prow, :] + ds)
    state_ref[...] = jnp.concatenate(new_state, axis=0)
    y_ssm = _ssm_gate_norm(jnp.concatenate(ys, axis=1), xs, buf["z"][rows, :], v512[0:1], v512[1:2])
    ymix_ref[rows, :SSM_WIDTH] = y_ssm.astype(BF16)
    yield

    if c == 0:
        kk = jnp.concatenate([kprev_ref[...], buf["kd"][rows, :]], axis=0)
        vv = jnp.concatenate([vprev_ref[...], buf["vd"][rows, :]], axis=0)
        thr = first_thr
    else:
        krows = slice((c - 1) * CHUNK, (c + 1) * CHUNK)
        kk = buf["kd"][krows, :]
        vv = buf["vd"][krows, :]
        thr = 0
    bias = jnp.concatenate([jnp.where((jj - ii) > thr, 0.0, NEG_INF),
                            jnp.where(tril, 0.0, NEG_INF)], axis=1)
    outs, scores = [], []
    for g in range(N_KV):
        lhs = []
        for pr in range(2):
            col = (2 * g + pr) * LANE
            lhs.append(buf["q"][rows, col:col + LANE])
            lhs.append(buf["q"][rows, ATT_WIDTH + col:ATT_WIDTH + col + LANE])
        scores.append(_dot_nt(jnp.concatenate(lhs, axis=0), kk[:, g * LANE:(g + 1) * LANE]))
    yield
    for g in range(N_KV):
        s = scores[g]
        ps, inv = [], []
        for hh in range(4):
            sink = sinks_ref[4 * g + hh]
            sh = s[hh * CHUNK:(hh + 1) * CHUNK] + bias
            m = jnp.maximum(jnp.max(sh, -1, keepdims=True), sink)
            p = jnp.exp(sh - m)
            inv.append(1.0 / (jnp.sum(p, -1, keepdims=True) + jnp.exp(sink - m)))
            ps.append(p.astype(BF16))
        o = _dot(jnp.concatenate(ps, axis=0), vv[:, g * LANE:(g + 1) * LANE])
        for pr in range(2):
            oa = o[(2 * pr) * CHUNK:(2 * pr + 1) * CHUNK] * inv[2 * pr]
            ob = o[(2 * pr + 1) * CHUNK:(2 * pr + 2) * CHUNK] * inv[2 * pr + 1]
            outs.append(jnp.where(lane < HEAD_DIM, oa, ob))
    ymix_ref[rows, SSM_WIDTH:] = jnp.concatenate(outs, axis=1).astype(BF16)
    yield


def _prompt_kernel(nt, nblk, ada_row0,
                   sinks_ref, x_ref, ada_ref, rot0_ref, rotd_ref, vec_ref, convw_ref, v512_ref,
                   v128_ref, wzx_ref, wqkv_ref, wdt_ref, wout_ref, wup_ref, wdn_ref,
                   y_ref, conv_ref, ssm_ref, kp_ref, vp_ref,
                   carry_ref, state_ref, kprev_ref, vprev_ref, xn_prev_ref, ymix_prev_ref, *buf_refs):
    i = pl.program_id(0)
    ia = jnp.minimum(i, nblk - 1)
    ib = jnp.maximum(i - 1, 0)
    ta = ia % nt
    buf = dict(zip([n for n, _, _ in _STAGE_BUFS], buf_refs))

    @pl.when(i == 0)
    def _():
        xn_prev_ref[...] = jnp.zeros_like(xn_prev_ref)
        ymix_prev_ref[...] = jnp.zeros_like(ymix_prev_ref)

    @pl.when(ta == 0)
    def _():
        carry_ref[...] = jnp.zeros_like(carry_ref)
        state_ref[...] = jnp.zeros_like(state_ref)
        kprev_ref[...] = jnp.zeros_like(kprev_ref)
        vprev_ref[...] = jnp.zeros_like(vprev_ref)

    def ada_of(blk):
        row = ada_row0 % SUBLANE + blk // nt
        return lambda j: ada_ref[j, pl.ds(row, 1), :]

    vec = vec_ref[...]
    res = {}
    out_gen = _stage_out(xn_prev_ref[...], ymix_prev_ref[...], ada_of(ib), vec, wout_ref, wup_ref, wdn_ref,
                         y_ref)

    def rot_tables():
        r0 = rot0_ref[ta]
        cos0, sin0 = r0[0:1], r0[1:2]
        cosd, sind = rotd_ref[0], rotd_ref[1]
        sin = sin0 * cosd + cos0 * sind
        return cos0 * cosd - sin0 * sind, sin * r0[2:3], sin * r0[3:4]

    def in_pieces():
        yield from _stage_in(x_ref[0], ada_of(ia), rot_tables, vec,
                             convw_ref, v128_ref, (wzx_ref, wqkv_ref, wdt_ref), carry_ref, buf, res)
        first_thr = jnp.where(ta > 0, 0, 4 * CHUNK)
        v512 = v512_ref[...]
        for c in range(TB // CHUNK):
            yield from _mix_chunk(c, buf, state_ref, kprev_ref, vprev_ref, sinks_ref,
                                  first_thr if c == 0 else 0, v512, ymix_prev_ref)
        kprev_ref[...] = buf["kd"][TB - CHUNK:, :]
        vprev_ref[...] = buf["vd"][TB - CHUNK:, :]
        yield

    def hand_over_xn():
        xn_prev_ref[...] = buf["xn"][...]
        yield

    gens = {"o": out_gen, "i": in_pieces(), "x": hand_over_xn()}
    for who in PIECE_ORDER:
        next(gens[who])
    for g in gens.values():
        assert next(g, "done") == "done"

    @pl.when((ta == nt - 1) & (i < nblk))
    def _():
        conv_ref[0] = res["tail"]
        kp_ref[0] = res["k"][TB - WINDOW:, :]
        vp_ref[0] = res["v"][TB - WINDOW:, :]
        ssm_ref[0] = state_ref[...]


def _prompt_call(x, ada_all, ada_row0, sinks, rot0, rotd, vec, convw, v512, v128, w):
    bsz, seq, _ = x.shape
    nt = seq // TB
    nblk = bsz * nt
    wzx, wqkv, wdt, wout, wup, wdn = w
    blk_a = lambda i: jnp.minimum(i, nblk - 1)
    blk_b = lambda i: jnp.maximum(i - 1, 0)
    in_specs = [
        pl.BlockSpec(memory_space=pltpu.SMEM),
        pl.BlockSpec((1, TB, D_MODEL), lambda i: (blk_a(i) // nt, blk_a(i) % nt, 0)),
        pl.BlockSpec((6, SUBLANE, D_MODEL), lambda i: (0, ada_row0 // SUBLANE, 0),
                     pipeline_mode=pl.Buffered(1)),
    ] + [_const_spec(a.shape) for a in (rot0, rotd, vec, convw, v512, v128, wzx, wqkv, wdt, wout, wup, wdn)]
    out_shape = (
        jax.ShapeDtypeStruct((bsz, seq, D_MODEL), F32),
        jax.ShapeDtypeStruct((bsz, SUBLANE, CONV_DIM), F32),
        jax.ShapeDtypeStruct((bsz, SSM_WIDTH, D_STATE), F32),
        jax.ShapeDtypeStruct((bsz, WINDOW, LANE), F32),
        jax.ShapeDtypeStruct((bsz, WINDOW, LANE), F32),
    )
    per_a = lambda shape: pl.BlockSpec((1,) + shape, lambda i: (blk_a(i) // nt, 0, 0))
    out_specs = (
        pl.BlockSpec((1, TB, D_MODEL), lambda i: (blk_b(i) // nt, blk_b(i) % nt, 0)),
        per_a((SUBLANE, CONV_DIM)),
        per_a((SSM_WIDTH, D_STATE)),
        per_a((WINDOW, LANE)),
        per_a((WINDOW, LANE)),
    )
    scratch = [
        pltpu.VMEM((SUBLANE, CONV_DIM), F32),
        pltpu.VMEM((SSM_WIDTH, D_STATE), F32),
        pltpu.VMEM((CHUNK, 2 * LANE), BF16),
        pltpu.VMEM((CHUNK, 2 * LANE), BF16),
        pltpu.VMEM((TB, D_MODEL), F32),
        pltpu.VMEM((TB, D_MODEL), BF16),
    ] + [pltpu.VMEM((TB, w_), d_) for _, w_, d_ in _STAGE_BUFS]
    return pl.pallas_call(
        functools.partial(_prompt_kernel, nt, nblk, ada_row0),
        grid=(nblk + 1,),
        in_specs=in_specs,
        out_specs=out_specs,
        out_shape=out_shape,
        scratch_shapes=scratch,
        compiler_params=pltpu.CompilerParams(
            dimension_semantics=("arbitrary",), vmem_limit_bytes=VMEM_LIMIT),
    )(sinks, x, ada_all, rot0, rotd, vec, convw, v512, v128, wzx, wqkv, wdt, wout, wup, wdn)


def _sample_kernel(sinks_ref, x_ref, cst_ref, ada_ref, h0_ref, kt_ref, vt_ref,
                   cos_ref, sa_ref, sb_ref, vec_ref, convw_ref, v512_ref, v128_ref, e_ref, et_ref,
                   wzx_ref, wqkv_ref, wdt_ref, wout_ref, wup_ref, wdn_ref,
                   y_ref, convn_ref, hn_ref, ko_ref, vo_ref):
    rows = SB * TILE_T
    vec = vec_ref[...]

    def ada(j):
        return jnp.broadcast_to(ada_ref[j], (SB, TILE_T, D_MODEL)).reshape(rows, D_MODEL)

    xn = _layer_norm(x_ref[...], vec[V_LN_IN_G:V_LN_IN_G + 1], vec[V_LN_IN_B:V_LN_IN_B + 1])
    h = (xn * (1.0 + ada(A_SC_M)) + ada(A_SH_M)).astype(BF16)
    zx = _dot(h, wzx_ref[...])
    qkv = _dot(h, wqkv_ref[...])
    dtr = _dot(h, wdt_ref[...])
    z = zx[:, :Z_END]
    xbc = zx[:, Z_END:]
    trow = lax.broadcasted_iota(jnp.int32, (rows, 1), 0) & (TILE_T - 1)
    is_tok = (trow >= TOK0) & (trow < TOK0 + 4)

    ext = jnp.where(is_tok, xbc, cst_ref[...])
    convn_ref[...] = pltpu.roll(ext, rows - (TOK0 + 1), 0)
    xbc_c = _conv_silu(ext, convw_ref[...], vec[V_CONV_B:V_CONV_B + 1], rows, 0)
    xs = xbc_c[:, :SSM_WIDTH]
    b_f = xbc_c[:, SSM_WIDTH:SSM_WIDTH + 2 * D_STATE]
    c_f = xbc_c[:, SSM_WIDTH + 2 * D_STATE:]
    b_bf = b_f.astype(BF16)
    c_bf = c_f.astype(BF16)

    v128 = v128_ref[...]
    dt = jnp.where(is_tok, _softplus(dtr + v128[0:1]), 0.0)
    cs = _seg_cumsum(dt * (-jnp.exp(v128[1:2])), TILE_T)
    cs3 = cs.reshape(SB, TILE_T, LANE)
    cs_end = jnp.broadcast_to(cs3[:, TOK0 + 3:TOK0 + 4, :], cs3.shape).reshape(rows, LANE)
    e = e_ref[...]
    xdt = xs * _sel_dot_r(dt, e, 2)
    ecs_exp = _sel_dot_r(jnp.exp(cs), e, 2)
    xd = xdt * _sel_dot_r(jnp.exp(cs_end - cs), e, 2)
    cs_t = cs.T

    ii = lax.broadcasted_iota(jnp.int32, (rows, rows), 0)
    jj = lax.broadcasted_iota(jnp.int32, (rows, rows), 1)
    same_causal = ((ii >> 3) == (jj >> 3)) & (jj <= ii)
    ydiag = _ssd_diag(c_bf, b_bf, xdt.astype(BF16), cs, cs_t, same_causal)

    half = SSM_WIDTH // SSM_GROUPS
    c3 = jnp.concatenate([c_f[:, g * D_STATE:(g + 1) * D_STATE].reshape(SB, TILE_T, D_STATE)
                          for g in range(SSM_GROUPS)], axis=1).astype(BF16)
    yy = jnp.einsum("bqd,bkd->bqk", c3, h0_ref[...].astype(BF16), preferred_element_type=F32)
    yoff = jnp.concatenate(
        [yy[:, g * TILE_T:(g + 1) * TILE_T, g * half:(g + 1) * half].reshape(rows, half)
         for g in range(SSM_GROUPS)], axis=1)
    v512 = v512_ref[...]
    y_ssm = _ssm_gate_norm(ydiag + yoff * ecs_exp, xs, z, v512[0:1], v512[1:2])

    et = et_ref[...]
    row_dec = None
    r = jnp.exp(cs_t)
    for _ in range(2):
        p = r.astype(BF16)
        t = _dot(et, p)
        row_dec = t if row_dec is None else row_dec + t
        r = r - p.astype(F32)
    brow = lax.broadcasted_iota(jnp.int32, (rows, 1), 0) >> 3
    xd_t = [xd[:, g * half:(g + 1) * half].T.astype(BF16) for g in range(SSM_GROUPS)]
    for bb in range(SB):
        col = bb * TILE_T + TOK0 + 3
        dec = jnp.broadcast_to(row_dec[:, col:col + 1], (SSM_WIDTH, D_STATE))
        ds = jnp.concatenate(
            [_dot(xd_t[g], jnp.where(brow == bb, b_f[:, g * D_STATE:(g + 1) * D_STATE], 0.0).astype(BF16))
             for g in range(SSM_GROUPS)], axis=0)
        hn_ref[bb] = dec * h0_ref[bb] + ds

    cos, sa, sb = cos_ref[...], sa_ref[...], sb_ref[...]
    q = _rotary(qkv[:, :ATT_WIDTH], cos, sa, sb) * (HEAD_DIM ** -0.5)
    k = _rotary(qkv[:, ATT_WIDTH:ATT_WIDTH + LANE], cos, sa, sb)
    v = qkv[:, ATT_WIDTH + LANE:]
    kn = [a.astype(BF16) for a in _dup_halves(k)]
    vn = [a.astype(BF16) for a in _dup_halves(v)]
    kt = kt_ref[...]
    vt = vt_ref[...]

    srow = 4 * rows
    r_i = lax.broadcasted_iota(jnp.int32, (srow, 1), 0)
    r_t = r_i & (TILE_T - 1)
    r_h = (r_i >> 3) & 3
    r_b = r_i >> 5
    j_c = lax.broadcasted_iota(jnp.int32, (srow, WINDOW), 1)
    cache_ok = j_c > (r_t - TOK0)
    j_n = lax.broadcasted_iota(jnp.int32, (srow, rows), 1)
    jn_t = j_n & (TILE_T - 1)
    new_ok = ((j_n >> 3) == r_b) & (jn_t <= r_t) & (jn_t >= TOK0) & (jn_t < TOK0 + 4)
    lo = lax.broadcasted_iota(jnp.int32, (rows, LANE), 1) < HEAD_DIM
    q_ab = _split_heads(q)
    outs = []
    for g in range(N_KV):
        gd = slice(g * HEAD_DIM, (g + 1) * HEAD_DIM)
        ktd = jnp.concatenate([kt[:, gd, :], kt[:, gd, :]], axis=1).astype(BF16)
        vtd = jnp.concatenate([vt[:, gd, :], vt[:, gd, :]], axis=1).astype(BF16)
        parts = []
        for pr in range(2):
            col = (2 * g + pr) * LANE
            parts.append(q_ab[:, col:col + LANE].reshape(SB, TILE_T, LANE))
            parts.append(q_ab[:, ATT_WIDTH + col:ATT_WIDTH + col + LANE].reshape(SB, TILE_T, LANE))
        q3 = jnp.concatenate(parts, axis=1)
        q2 = q3.reshape(srow, LANE).astype(BF16)
        s_c = jnp.einsum("bqd,bdk->bqk", q3.astype(BF16), ktd,
                         preferred_element_type=F32).reshape(srow, WINDOW)
        s_n = _dot_nt(q2, kn[g])
        s_c = jnp.where(cache_ok, s_c, NEG_INF)
        s_n = jnp.where(new_ok, s_n, NEG_INF)
        sink = jnp.where(r_h == 0, sinks_ref[4 * g],
                         jnp.where(r_h == 1, sinks_ref[4 * g + 1],
                                   jnp.where(r_h == 2, sinks_ref[4 * g + 2], sinks_ref[4 * g + 3])))
        m = jnp.maximum(jnp.maximum(jnp.max(s_c, -1, keepdims=True), jnp.max(s_n, -1, keepdims=True)), sink)
        p_c = jnp.exp(s_c - m)
        p_n = jnp.exp(s_n - m)
        den = jnp.sum(p_c, -1, keepdims=True) + jnp.sum(p_n, -1, keepdims=True) + jnp.exp(sink - m)
        o = jnp.einsum("bqk,bdk->bqd", p_c.reshape(SB, 4 * TILE_T, WINDOW).astype(BF16), vtd,
                       preferred_element_type=F32).reshape(srow, LANE)
        o = (o + _dot(p_n.astype(BF16), vn[g])) * (1.0 / den)
        o3 = o.reshape(SB, 4 * TILE_T, LANE)
        for pr in range(2):
            oa = o3[:, (2 * pr) * TILE_T:(2 * pr + 1) * TILE_T, :].reshape(rows, LANE)
            ob = o3[:, (2 * pr + 1) * TILE_T:(2 * pr + 2) * TILE_T, :].reshape(rows, LANE)
            outs.append(jnp.where(lo, oa, ob))
    ymix = jnp.concatenate([y_ssm] + outs, axis=1).astype(BF16)

    lane_w = lax.broadcasted_iota(jnp.int32, (LANE, WINDOW), 1)
    for src, new, dst in ((kt, k, ko_ref), (vt, v, vo_ref)):
        sh = pltpu.roll(src.reshape(SB * LANE, WINDOW), WINDOW - DEC_T, 1).reshape(SB, LANE, WINDOW)
        new_t = new.T
        for bb in range(SB):
            col = pltpu.roll(new_t, (WINDOW - DEC_T - (bb * TILE_T + TOK0)) % rows, 1)
            dst[bb] = jnp.where(lane_w >= WINDOW - DEC_T, col, sh[bb])

    for _ in _stage_out(xn, ymix, ada, vec, wout_ref, wup_ref, wdn_ref, y_ref):
        pass


def _sample_call(sinks, x_s, cst, ada4, h0, kt, vt, tabs, vec, convw, v512, v128, e, et, w):
    nb = h0.shape[0]
    rows = SB * TILE_T
    cos, sa, sb = tabs
    wzx, wqkv, wdt, wout, wup, wdn = w
    row_spec = lambda wd: pl.BlockSpec((rows, wd), lambda i: (i, 0))
    per_b = lambda shape: pl.BlockSpec((SB,) + shape, lambda i: (i,) + (0,) * len(shape))
    in_specs = [
        pl.BlockSpec(memory_space=pltpu.SMEM),
        row_spec(D_MODEL), row_spec(CONV_DIM),
        pl.BlockSpec((6, SB, 1, D_MODEL), lambda i: (0, i, 0, 0)),
        per_b((SSM_WIDTH, D_STATE)), per_b((LANE, WINDOW)), per_b((LANE, WINDOW)),
    ] + [_const_spec(a.shape) for a in (cos, sa, sb, vec, convw, v512, v128, e, et,
                                        wzx, wqkv, wdt, wout, wup, wdn)]
    out_specs = (row_spec(D_MODEL), row_spec(CONV_DIM),
                 per_b((SSM_WIDTH, D_STATE)), per_b((LANE, WINDOW)), per_b((LANE, WINDOW)))
    out_shape = (
        jax.ShapeDtypeStruct((nb * TILE_T, D_MODEL), F32),
        jax.ShapeDtypeStruct((nb * TILE_T, CONV_DIM), F32),
        jax.ShapeDtypeStruct((nb, SSM_WIDTH, D_STATE), F32),
        jax.ShapeDtypeStruct((nb, LANE, WINDOW), F32),
        jax.ShapeDtypeStruct((nb, LANE, WINDOW), F32),
    )
    return pl.pallas_call(
        _sample_kernel,
        grid=(nb // SB,),
        in_specs=in_specs,
        out_specs=out_specs,
        out_shape=out_shape,
        compiler_params=pltpu.CompilerParams(
            dimension_semantics=("arbitrary",), vmem_limit_bytes=VMEM_LIMIT),
    )(sinks, x_s, cst, ada4, h0, kt, vt, cos, sa, sb, vec, convw, v512, v128, e, et,
      wzx, wqkv, wdt, wout, wup, wdn)


def _lane_freq():
    half = ROT_DIM // 2
    inv = ROPE_THETA ** (-jnp.arange(half, dtype=F32) / half)
    d = np.arange(LANE) % HEAD_DIM
    rot = d < ROT_DIM
    inv_lane = jnp.where(rot, inv[np.where(rot, d % half, 0)], 0.0)
    mask_a = jnp.asarray(((d >= half) & rot).astype(np.float32))
    mask_b = jnp.asarray(-(d < half).astype(np.float32))
    return inv_lane, mask_a, mask_b


def _rot_tables(pos):
    inv_lane, mask_a, mask_b = _lane_freq()
    ang = pos.astype(F32)[:, None] * inv_lane[None, :]
    sin = jnp.sin(ang)
    return jnp.cos(ang), sin * mask_a, sin * mask_b


def _rot_parts(nt):
    inv_lane, mask_a, mask_b = _lane_freq()
    ang0 = (jnp.arange(nt, dtype=jnp.int32) * TB).astype(F32)[:, None] * inv_lane[None, :]
    angd = jnp.arange(TB, dtype=jnp.int32).astype(F32)[:, None] * inv_lane[None, :]
    zero = jnp.zeros((nt, LANE), F32)
    rot0 = jnp.stack([jnp.cos(ang0), jnp.sin(ang0), jnp.broadcast_to(mask_a, (nt, LANE)),
                      jnp.broadcast_to(mask_b, (nt, LANE)), zero, zero, zero, zero], axis=1)
    return rot0, jnp.stack([jnp.cos(angd), jnp.sin(angd)])


def _head_expanders():
    e = np.zeros((LANE, SSM_WIDTH), np.float32)
    ch = np.arange(SSM_WIDTH)
    e[ch // SSM_HEAD_DIM, ch] = 1.0
    return jnp.asarray(e, BF16), jnp.asarray(e.T.copy(), BF16)


def kernel(x_prompt, x_sample, state_conv, state_ssm, cache_k, cache_v, c_prompt, c_sample, ln_in_g, ln_in_b, w_ada, b_ada, w_in, conv_w, conv_b, dt_bias, A_log, D_skip, ssm_norm_w, sinks, w_out, ln1_g, ln1_b, w_up, w_down, ln2_g, ln2_b):
    bsz, seq, _ = x_prompt.shape
    nb, dec_t, _ = x_sample.shape
    assert seq % TB == 0 and nb % SB == 0
    assert dec_t == DEC_T and bsz <= SUBLANE and nb % SUBLANE == 0 and cache_k.shape[2] == WINDOW

    w_in0 = w_in[0]
    wzx = w_in0[:, :XBC_END].astype(BF16)
    wdt = jnp.pad(w_in0[:, XBC_END:DT_END], ((0, 0), (0, LANE - SSM_HEADS))).astype(BF16)
    wqkv = w_in0[:, DT_END:].astype(BF16)
    w = (wzx, wqkv, wdt, w_out[0].astype(BF16), w_up[0].astype(BF16), w_down[0].astype(BF16))
    vec = jnp.stack([ln_in_g, ln_in_b, ln1_g[0], ln1_b[0], ln2_g[0], ln2_b[0], conv_b[0],
                     jnp.zeros((D_MODEL,), F32)])
    convw = jnp.pad(conv_w[0], ((0, SUBLANE - CONV_W), (0, 0)))
    v512 = jnp.pad(jnp.stack([jnp.repeat(D_skip[0], SSM_HEAD_DIM), ssm_norm_w[0]]),
                   ((0, SUBLANE - 2), (0, 0)))
    v128 = jnp.pad(jnp.stack([dt_bias[0], A_log[0]]), ((0, SUBLANE - 2), (0, LANE - SSM_HEADS)))
    sinks0 = sinks[0]

    c_all = jnp.concatenate([c_sample, c_prompt, jnp.zeros((SUBLANE - bsz, D_MODEL), F32)], axis=0)
    ada_all = _ada_call(c_all, w_ada[0], b_ada)

    rot0, rotd = _rot_parts(seq // TB)
    y_p, conv_p8, ssm_p, k_p, v_p = _prompt_call(
        x_prompt, ada_all, nb, sinks0, rot0, rotd, vec, convw, v512, v128, w)

    rows_s = nb * TILE_T
    x_s = jnp.pad(x_sample, ((0, 0), (TOK0, TILE_T - TOK0 - dec_t), (0, 0))).reshape(rows_s, D_MODEL)
    cst = jnp.pad(state_conv[0], ((0, 0), (0, TILE_T - (CONV_W - 1)), (0, 0))).reshape(rows_s, CONV_DIM)
    tile_pos = jnp.clip(jnp.arange(TILE_T, dtype=jnp.int32) - TOK0, 0, dec_t - 1) + PAST_LEN
    tabs_s = tuple(jnp.tile(t, (SB, 1)) for t in _rot_tables(tile_pos))
    e, et = _head_expanders()
    cache_t = lambda c: jnp.transpose(c[0], (0, 2, 3, 1)).reshape(nb, LANE, WINDOW)
    y_s, convn_s, ssm_s, kt_s, vt_s = _sample_call(
        sinks0, x_s, cst, ada_all[:, :nb].reshape(6, nb, 1, D_MODEL),
        state_ssm[0].reshape(nb, SSM_WIDTH, D_STATE), cache_t(cache_k), cache_t(cache_v),
        tabs_s, vec, convw, v512, v128, e, et, w)
    cache_back = lambda c: jnp.transpose(c.reshape(nb, N_KV, HEAD_DIM, WINDOW), (0, 3, 1, 2))[None]

    return (
        y_p,
        y_s.reshape(nb, TILE_T, D_MODEL)[:, TOK0:TOK0 + dec_t],
        conv_p8[:, SUBLANE - (CONV_W - 1):][None],
        ssm_p.reshape(1, bsz, SSM_HEADS, SSM_HEAD_DIM, D_STATE),
        k_p.reshape(1, bsz, WINDOW, N_KV, HEAD_DIM),
        v_p.reshape(1, bsz, WINDOW, N_KV, HEAD_DIM),
        convn_s.reshape(nb, TILE_T, CONV_DIM)[:, :CONV_W - 1][None],
        ssm_s.reshape(1, nb, SSM_HEADS, SSM_HEAD_DIM, D_STATE),
        cache_back(kt_s),
        cache_back(vt_s),
    )
```
